```python
import jax, jax.numpy as jnp
from jax import lax
import numpy as np

D_MODEL = 1024
BATCH = 8
SEQ = 2048
DEPTH = 1
DEC_BATCH = 128
DEC_SEQ = 4
PAST_LEN = 16384
PAGE_SIZE = 128

N_META = 16
MIX_WIDTH = D_MODEL
C_CONV = MIX_WIDTH // 2
C_POOL = MIX_WIDTH - C_CONV
CONV_HEADS = 8
CONV_WIDTH = 31
CONV_HIST = CONV_WIDTH - 1
POOL_WINDOWS = (2, 4, 8, 16)
N_POOL_GROUPS = len(POOL_WINDOWS)
POOL_GROUP = C_POOL // N_POOL_GROUPS
POOL_HIST = max(POOL_WINDOWS) - 1
D_IN = 2 * C_CONV + C_POOL
D_FF = ((8 * D_MODEL // 3 + 127) // 128) * 128
EPS = 1e-6

kernel_name = "hymba_conv_pool_macaron_step"


def rmsnorm(x, g):
    xf = x.astype(jnp.float32)
    y = xf * lax.rsqrt(jnp.mean(xf * xf, axis=-1, keepdims=True) + EPS)
    return (y * g.astype(jnp.float32)).astype(x.dtype)


def layernorm(x, g, b):
    xf = x.astype(jnp.float32)
    mu = jnp.mean(xf, axis=-1, keepdims=True)
    xc = xf - mu
    var = jnp.mean(xc * xc, axis=-1, keepdims=True)
    y = xc * lax.rsqrt(var + EPS) * g.astype(jnp.float32) + b.astype(jnp.float32)
    return y.astype(x.dtype)


def swiglu(x, wg, wu, wd):
    return (jax.nn.silu(x @ wg) * (x @ wu)) @ wd


def depthwise_causal_conv(u_ext, w, b):
    out = lax.conv_general_dilated(
        u_ext, w.astype(u_ext.dtype)[:, None, :], window_strides=(1,), padding="VALID",
        dimension_numbers=("NWC", "WIO", "NWC"), feature_group_count=u_ext.shape[-1])
    return out + b.astype(u_ext.dtype)


def multiscale_pool(p_ext, pos0, w_lin, scale):
    bsz, L, _ = p_ext.shape
    T = L - POOL_HIST
    pf = p_ext.astype(jnp.float32)
    cs = jnp.concatenate([jnp.zeros_like(pf[:, :1]), jnp.cumsum(pf, axis=1)], axis=1)
    end = cs[:, POOL_HIST + 1:]
    pos = pos0 + jnp.arange(T)
    means = []
    for g, w in enumerate(POOL_WINDOWS):
        sl = slice(g * POOL_GROUP, (g + 1) * POOL_GROUP)
        start = cs[:, POOL_HIST + 1 - w: POOL_HIST + 1 - w + T, sl]
        cnt = jnp.minimum(pos + 1, w).astype(jnp.float32)[None, :, None]
        means.append((end[..., sl] - start) / cnt)
    d = jnp.concatenate(means, axis=-1) - pf[:, POOL_HIST:]
    d = d.reshape(bsz, T, N_POOL_GROUPS, POOL_GROUP)
    y = jnp.einsum("btgc,gcd->btgd", d, w_lin.astype(jnp.float32)).reshape(bsz, T, C_POOL)
    return (y * scale.astype(jnp.float32)).astype(p_ext.dtype)


def hybrid_layer(x, conv_prev, pool_prev, pos0,
                 g1, w1g, w1u, w1d, gm, w_in, b_in, w_dw, b_dw, ln_g, ln_b,
                 w_pool, pool_scale, w_out, b_out, g2, w2g, w2u, w2d):
    x = x + 0.5 * swiglu(rmsnorm(x, g1), w1g, w1u, w1d)
    h = rmsnorm(x, gm)
    z = h @ w_in + b_in
    u = z[..., :C_CONV] * jax.nn.sigmoid(z[..., C_CONV:2 * C_CONV])
    p_in = z[..., 2 * C_CONV:]
    u_ext = jnp.concatenate([conv_prev.astype(u.dtype), u], axis=1)
    c = depthwise_causal_conv(u_ext, w_dw, b_dw)
    c = jax.nn.silu(layernorm(c, ln_g, ln_b))
    p_ext = jnp.concatenate([pool_prev.astype(p_in.dtype), p_in], axis=1)
    q = multiscale_pool(p_ext, pos0, w_pool, pool_scale)
    x = x + jnp.concatenate([c, q], axis=-1) @ w_out + b_out
    x = x + 0.5 * swiglu(rmsnorm(x, g2), w2g, w2u, w2d)
    return x, u_ext[:, -CONV_HIST:], p_ext[:, -POOL_HIST:]


def setup_inputs(seed: int = 0) -> dict:
    key = jax.random.key(seed)
    ks = jax.random.split(key, 24)
    n = jax.random.normal
    f = jnp.float32

    def gain(k, shape):
        return 1.0 + 0.05 * n(k, shape, f)

    return {
        "x_prompt": n(ks[0], (BATCH, SEQ, D_MODEL), f),
        "x_sample": n(ks[1], (DEC_BATCH, DEC_SEQ, D_MODEL), f),
        "state_conv": 0.5 * n(ks[2], (DEPTH, DEC_BATCH, CONV_HIST, C_CONV), f),
        "state_pool": n(ks[3], (DEPTH, DEC_BATCH, POOL_HIST, C_POOL), f),
        "meta_tokens": n(ks[4], (N_META, D_MODEL), f),
        "norm_ffn1": gain(ks[5], (DEPTH, D_MODEL)),
        "w_ffn1_gate": n(ks[6], (DEPTH, D_MODEL, D_FF), f) * D_MODEL ** -0.5,
        "w_ffn1_up": n(ks[7], (DEPTH, D_MODEL, D_FF), f) * D_MODEL ** -0.5,
        "w_ffn1_down": n(ks[8], (DEPTH, D_FF, D_MODEL), f) * D_FF ** -0.5,
        "norm_mix": gain(ks[9], (DEPTH, D_MODEL)),
        "w_in": n(ks[10], (DEPTH, D_MODEL, D_IN), f) * D_MODEL ** -0.5,
        "b_in": 0.02 * n(ks[11], (DEPTH, D_IN), f),
        "w_dw": n(ks[12], (DEPTH, CONV_WIDTH, C_CONV), f) * CONV_WIDTH ** -0.5,
        "b_dw": 0.02 * n(ks[13], (DEPTH, C_CONV), f),
        "ln_conv_g": gain(ks[14], (DEPTH, C_CONV)),
        "ln_conv_b": 0.02 * n(ks[15], (DEPTH, C_CONV), f),
        "w_pool": n(ks[16], (DEPTH, N_POOL_GROUPS, POOL_GROUP, POOL_GROUP), f) * POOL_GROUP ** -0.5,
        "pool_scale": gain(ks[17], (DEPTH, C_POOL)),
        "w_out": n(ks[18], (DEPTH, MIX_WIDTH, D_MODEL), f) * MIX_WIDTH ** -0.5,
        "b_out": 0.02 * n(ks[19], (DEPTH, D_MODEL), f),
        "norm_ffn2": gain(ks[20], (DEPTH, D_MODEL)),
        "w_ffn2_gate": n(ks[21], (DEPTH, D_MODEL, D_FF), f) * D_MODEL ** -0.5,
        "w_ffn2_up": n(ks[22], (DEPTH, D_MODEL, D_FF), f) * D_MODEL ** -0.5,
        "w_ffn2_down": n(ks[23], (DEPTH, D_FF, D_MODEL), f) * D_FF ** -0.5,
        "norm_final": gain(jax.random.fold_in(key, 99), (D_MODEL,)),
    }


def reference(x_prompt, x_sample, state_conv, state_pool, meta_tokens,
              norm_ffn1, w_ffn1_gate, w_ffn1_up, w_ffn1_down,
              norm_mix, w_in, b_in, w_dw, b_dw, ln_conv_g, ln_conv_b,
              w_pool, pool_scale, w_out, b_out,
              norm_ffn2, w_ffn2_gate, w_ffn2_up, w_ffn2_down, norm_final):
    bsz = x_prompt.shape[0]
    meta = jnp.broadcast_to(meta_tokens.astype(x_prompt.dtype)[None], (bsz, N_META, D_MODEL))
    xp = jnp.concatenate([meta, x_prompt], axis=1)
    xs = x_sample
    conv_p, pool_p, conv_s, pool_s = [], [], [], []
    for l in range(DEPTH):
        params = (norm_ffn1[l], w_ffn1_gate[l], w_ffn1_up[l], w_ffn1_down[l],
                  norm_mix[l], w_in[l], b_in[l], w_dw[l], b_dw[l], ln_conv_g[l], ln_conv_b[l],
                  w_pool[l], pool_scale[l], w_out[l], b_out[l],
                  norm_ffn2[l], w_ffn2_gate[l], w_ffn2_up[l], w_ffn2_down[l])
        zc = jnp.zeros((bsz, CONV_HIST, C_CONV), xp.dtype)
        zp = jnp.zeros((bsz, POOL_HIST, C_POOL), xp.dtype)
        xp, cpn, ppn = hybrid_layer(xp, zc, zp, 0, *params)
        xs, csn, psn = hybrid_layer(xs, state_conv[l], state_pool[l], PAST_LEN, *params)
        conv_p.append(cpn)
        pool_p.append(ppn)
        conv_s.append(csn)
        pool_s.append(psn)
    y_prompt = rmsnorm(xp, norm_final)[:, N_META:]
    y_sample = rmsnorm(xs, norm_final)
    new_conv_prompt = jnp.stack(conv_p, axis=0)
    new_pool_prompt = jnp.stack(pool_p, axis=0)
    new_conv_sample = jnp.stack(conv_s, axis=0)
    new_pool_sample = jnp.stack(pool_s, axis=0)
    return (y_prompt, y_sample, new_conv_prompt, new_pool_prompt, new_conv_sample, new_pool_sample)
```

```python
import functools

import jax
import jax.numpy as jnp
from jax import lax
from jax.experimental import pallas as pl
from jax.experimental.pallas import tpu as pltpu

F32 = jnp.float32
BF16 = jnp.bfloat16

D_MODEL = 1024
N_META = 16
C_CONV = 512
C_POOL = 512
CONV_WIDTH = 31
CONV_HIST = CONV_WIDTH - 1
POOL_WINDOWS = (2, 4, 8, 16)
POOL_HIST = max(POOL_WINDOWS) - 1
D_IN = 2 * C_CONV + C_POOL
EPS = 1e-6

LANES = 128
N_SLAB = C_CONV // LANES
U_PAD = 32
P_PAD = 16
CONV_ROWS = 128

TM_FFN = 512
TM_MIX = 512
VMEM_LIMIT = 56 * 1024 * 1024


def _rmsnorm(x, g):
    ms = jnp.mean(x * x, axis=-1, keepdims=True)
    return x * lax.rsqrt(ms + EPS) * g


def _dot(a, b):
    return jnp.dot(a, b, preferred_element_type=F32)


def _ffn_kernel(x_ref, g_ref, wg_ref, wu_ref, wd_ref, *rest, final_norm):
    x = x_ref[...]
    h = _rmsnorm(x, g_ref[...]).astype(BF16)
    a = _dot(h, wg_ref[...])
    b = _dot(h, wu_ref[...])
    act = (a * jax.nn.sigmoid(a) * b).astype(BF16)
    y = x + 0.5 * _dot(act, wd_ref[...])
    if final_norm:
        gf_ref, o_ref = rest
        y = _rmsnorm(y, gf_ref[...])
    else:
        (o_ref,) = rest
    o_ref[...] = y


def _const_spec(shape):
    nd = len(shape)
    return pl.BlockSpec(shape, lambda *_: (0,) * nd, pipeline_mode=pl.Buffered(1))


def _ffn(x, g, wg, wu, wd, gf=None, *, tm, name):
    rows = x.shape[0]
    assert rows % tm == 0
    d_ff = wg.shape[1]
    in_specs = [
        pl.BlockSpec((tm, D_MODEL), lambda i: (i, 0)),
        _const_spec((1, D_MODEL)),
        _const_spec((D_MODEL, d_ff)),
        _const_spec((D_MODEL, d_ff)),
        _const_spec((d_ff, D_MODEL)),
    ]
    args = [x, g, wg, wu, wd]
    if gf is not None:
        in_specs.append(_const_spec((1, D_MODEL)))
        args.append(gf)
    return pl.pallas_call(
        functools.partial(_ffn_kernel, final_norm=gf is not None),
        grid=(rows // tm,),
        in_specs=in_specs,
        out_specs=pl.BlockSpec((tm, D_MODEL), lambda i: (i, 0)),
        out_shape=jax.ShapeDtypeStruct((rows, D_MODEL), F32),
        compiler_params=pltpu.CompilerParams(
            dimension_semantics=("arbitrary",), vmem_limit_bytes=VMEM_LIMIT),
        name=name,
    )(*args)


def _in_proj(x1, gm_ref, w_in_ref, b_in_ref):
    h = _rmsnorm(x1, gm_ref[...]).astype(BF16)
    z = _dot(h, w_in_ref[...]) + b_in_ref[...]
    u = z[:, :C_CONV] * jax.nn.sigmoid(z[:, C_CONV:2 * C_CONV])
    return u, z[:, 2 * C_CONV:]


def _ln_silu(c, g_ref, b_ref):
    mu = jnp.mean(c, axis=-1, keepdims=True)
    xc = c - mu
    var = jnp.mean(xc * xc, axis=-1, keepdims=True)
    y = xc * lax.rsqrt(var + EPS) * g_ref[...] + b_ref[...]
    return y * jax.nn.sigmoid(y)


def _pool_linear(d_slabs, w_pool_ref, scale_ref):
    outs = []
    for pair in range(N_SLAB // 2):
        d = jnp.concatenate(d_slabs[2 * pair:2 * pair + 2], axis=-1).astype(BF16)
        outs.append(_dot(d, w_pool_ref[pair]))
    return jnp.concatenate(outs, axis=-1) * scale_ref[...]


def _mix_prompt_kernel(x_ref, xm_ref, gm_ref, w_in_ref, b_in_ref, w_dw_ref, b_dw_ref, ln_g_ref, ln_b_ref,
                       w_pool_ref, scale_ref, w_out_ref, b_out_ref,
                       o_ref, nc_ref, np_ref,
                       u_ext, p_ext, meta_u, meta_p, c_scr, cq_scr, *, tm):
    b = pl.program_id(0)
    t = pl.program_id(1)

    @pl.when(jnp.logical_and(b == 0, t == 0))
    def _():
        um, pm = _in_proj(xm_ref[...], gm_ref, w_in_ref, b_in_ref)
        for j in range(N_SLAB):
            sl = pl.ds(LANES * j, LANES)
            meta_u[j, pl.ds(0, U_PAD - N_META), :] = jnp.zeros((U_PAD - N_META, LANES), F32)
            meta_u[j, pl.ds(U_PAD - N_META, N_META), :] = um[:, LANES * j:LANES * (j + 1)]
            meta_p[j] = pm[:, LANES * j:LANES * (j + 1)]
            del sl

    @pl.when(t == 0)
    def _():
        for j in range(N_SLAB):
            u_ext[j, pl.ds(0, U_PAD), :] = meta_u[j]
            p_ext[j, pl.ds(0, P_PAD), :] = meta_p[j]

    x1 = x_ref[...]
    u, p = _in_proj(x1, gm_ref, w_in_ref, b_in_ref)
    for j in range(N_SLAB):
        u_ext[j, pl.ds(U_PAD, tm), :] = u[:, LANES * j:LANES * (j + 1)]
        p_ext[j, pl.ds(P_PAD, tm), :] = p[:, LANES * j:LANES * (j + 1)]

    for j in range(N_SLAB):
        lanes = pl.ds(LANES * j, LANES)
        for r0 in range(0, tm, CONV_ROWS):
            acc = jnp.broadcast_to(b_dw_ref[:, lanes], (CONV_ROWS, LANES))
            for k in range(CONV_WIDTH):
                acc = acc + u_ext[j, pl.ds(U_PAD - CONV_HIST + r0 + k, CONV_ROWS), :] * w_dw_ref[pl.ds(k, 1), lanes]
            c_scr[pl.ds(r0, CONV_ROWS), lanes] = acc
    cq_scr[:, pl.ds(0, C_CONV)] = _ln_silu(c_scr[...], ln_g_ref, ln_b_ref).astype(BF16)

    d_slabs = []
    for j, w in enumerate(POOL_WINDOWS):
        s = p_ext[j, pl.ds(P_PAD, tm), :]
        cur = s
        for i in range(1, w):
            s = s + p_ext[j, pl.ds(P_PAD - i, tm), :]
        d_slabs.append(s * (1.0 / w) - cur)
    cq_scr[:, pl.ds(C_CONV, C_POOL)] = _pool_linear(d_slabs, w_pool_ref, scale_ref).astype(BF16)

    o_ref[...] = x1 + _dot(cq_scr[...], w_out_ref[...]) + b_out_ref[...]

    for j in range(N_SLAB):
        lanes = pl.ds(LANES * j, LANES)
        u_ext[j, pl.ds(0, U_PAD), :] = u_ext[j, pl.ds(tm, U_PAD), :]
        p_ext[j, pl.ds(0, P_PAD), :] = p_ext[j, pl.ds(tm, P_PAD), :]
        nc_ref[:, lanes] = u_ext[j, pl.ds(U_PAD - CONV_HIST, CONV_HIST), :]
        np_ref[:, lanes] = p_ext[j, pl.ds(P_PAD - POOL_HIST, POOL_HIST), :]


def _mix_prompt(x1, x1_meta, mw, *, tm):
    bsz, seq, _ = x1.shape
    assert seq % tm == 0
    tile = pl.BlockSpec((None, tm, D_MODEL), lambda b, t: (b, t, 0))
    in_specs = [tile, _const_spec((N_META, D_MODEL))] + [_const_spec(w.shape) for w in mw]
    return pl.pallas_call(
        functools.partial(_mix_prompt_kernel, tm=tm),
        grid=(bsz, seq // tm),
        in_specs=in_specs,
        out_specs=[
            tile,
            pl.BlockSpec((None, CONV_HIST, C_CONV), lambda b, t: (b, 0, 0)),
            pl.BlockSpec((None, POOL_HIST, C_POOL), lambda b, t: (b, 0, 0)),
        ],
        out_shape=[
            jax.ShapeDtypeStruct((bsz, seq, D_MODEL), F32),
            jax.ShapeDtypeStruct((bsz, CONV_HIST, C_CONV), F32),
            jax.ShapeDtypeStruct((bsz, POOL_HIST, C_POOL), F32),
        ],
        scratch_shapes=[
            pltpu.VMEM((N_SLAB, U_PAD + tm, LANES), F32),
            pltpu.VMEM((N_SLAB, P_PAD + tm, LANES), F32),
            pltpu.VMEM((N_SLAB, U_PAD, LANES), F32),
            pltpu.VMEM((N_SLAB, P_PAD, LANES), F32),
            pltpu.VMEM((tm, C_CONV), F32),
            pltpu.VMEM((tm, C_CONV + C_POOL), BF16),
        ],
        compiler_params=pltpu.CompilerParams(
            dimension_semantics=("arbitrary", "arbitrary"), vmem_limit_bytes=VMEM_LIMIT),
        name="mix_prompt",
    )(x1, x1_meta, *mw)


def _mix_sample_kernel(x_ref, sc_ref, sp_ref, gm_ref, w_in_ref, b_in_ref, w_dw_ref, b_dw_ref, ln_g_ref, ln_b_ref,
                       w_pool_ref, scale_ref, w_out_ref, b_out_ref,
                       o_ref, nc_ref, np_ref, c_scr, *, nb, nt):
    x1 = x_ref[...]
    u, p = _in_proj(x1, gm_ref, w_in_ref, b_in_ref)

    for j in range(N_SLAB):
        lanes = pl.ds(LANES * j, LANES)
        lo, hi = LANES * j, LANES * (j + 1)
        for t in range(nt):
            acc = jnp.broadcast_to(b_dw_ref[:, lanes], (nb, LANES))
            for m in range(t, CONV_HIST):
                acc = acc + sc_ref[m, :, lanes] * w_dw_ref[pl.ds(m - t, 1), lanes]
            for s in range(t + 1):
                acc = acc + u[s * nb:(s + 1) * nb, lo:hi] * w_dw_ref[pl.ds(CONV_HIST - t + s, 1), lanes]
            c_scr[pl.ds(t * nb, nb), lanes] = acc
    c_act = _ln_silu(c_scr[...], ln_g_ref, ln_b_ref).astype(BF16)

    d_slabs = []
    for j, w in enumerate(POOL_WINDOWS):
        lanes = pl.ds(LANES * j, LANES)
        lo, hi = LANES * j, LANES * (j + 1)
        rows = []
        for t in range(nt):
            cur = p[t * nb:(t + 1) * nb, lo:hi]
            s = cur
            for i in range(1, w):
                e = POOL_HIST + t - i
                if e >= POOL_HIST:
                    s = s + p[(e - POOL_HIST) * nb:(e - POOL_HIST + 1) * nb, lo:hi]
                else:
                    s = s + sp_ref[e, :, lanes]
            rows.append(s * (1.0 / w) - cur)
        d_slabs.append(jnp.concatenate(rows, axis=0))
    q = _pool_linear(d_slabs, w_pool_ref, scale_ref).astype(BF16)

    cq = jnp.concatenate([c_act, q], axis=-1)
    o_ref[...] = x1 + _dot(cq, w_out_ref[...]) + b_out_ref[...]

    for m in range(CONV_HIST - nt):
        nc_ref[m] = sc_ref[m + nt]
    for m in range(POOL_HIST - nt):
        np_ref[m] = sp_ref[m + nt]
    for t in range(nt):
        nc_ref[CONV_HIST - nt + t] = u[t * nb:(t + 1) * nb, :]
        np_ref[POOL_HIST - nt + t] = p[t * nb:(t + 1) * nb, :]


def _mix_sample(x1_tm, sc_tm, sp_tm, mw, *, nb, nt):
    rows = nb * nt
    args = [x1_tm, sc_tm, sp_tm] + list(mw)
    return pl.pallas_call(
        functools.partial(_mix_sample_kernel, nb=nb, nt=nt),
        grid=(1,),
        in_specs=[_const_spec(a.shape) for a in args],
        out_specs=[_const_spec((rows, D_MODEL)), _const_spec(sc_tm.shape), _const_spec(sp_tm.shape)],
        out_shape=[
            jax.ShapeDtypeStruct((rows, D_MODEL), F32),
            jax.ShapeDtypeStruct(sc_tm.shape, F32),
            jax.ShapeDtypeStruct(sp_tm.shape, F32),
        ],
        scratch_shapes=[pltpu.VMEM((rows, C_CONV), F32)],
        compiler_params=pltpu.CompilerParams(
            dimension_semantics=("arbitrary",), vmem_limit_bytes=VMEM_LIMIT),
        name="mix_sample",
    )(*args)


def _pool_block_diag(w_pool):
    g = w_pool.shape[1]
    z = jnp.zeros((g, g), w_pool.dtype)
    pairs = [jnp.block([[w_pool[2 * i], z], [z, w_pool[2 * i + 1]]]) for i in range(w_pool.shape[0] // 2)]
    return jnp.stack(pairs, axis=0)


def kernel(x_prompt, x_sample, state_conv, state_pool, meta_tokens, norm_ffn1, w_ffn1_gate, w_ffn1_up, w_ffn1_down, norm_mix, w_in, b_in, w_dw, b_dw, ln_conv_g, ln_conv_b, w_pool, pool_scale, w_out, b_out, norm_ffn2, w_ffn2_gate, w_ffn2_up, w_ffn2_down, norm_final):
    depth = norm_ffn1.shape[0]
    assert depth == 1
    bsz, seq, _ = x_prompt.shape
    nb, nt, _ = x_sample.shape
    row = lambda v: v.reshape(1, -1).astype(F32)

    xp = x_prompt.reshape(bsz * seq, D_MODEL)
    xs = jnp.concatenate([x_sample.transpose(1, 0, 2).reshape(nb * nt, D_MODEL), meta_tokens.astype(F32)], axis=0)
    sc_tm = state_conv[0].transpose(1, 0, 2)
    sp_tm = state_pool[0].transpose(1, 0, 2)

    l = 0
    ffn1 = (row(norm_ffn1[l]), w_ffn1_gate[l].astype(BF16), w_ffn1_up[l].astype(BF16), w_ffn1_down[l].astype(BF16))
    ffn2 = (row(norm_ffn2[l]), w_ffn2_gate[l].astype(BF16), w_ffn2_up[l].astype(BF16), w_ffn2_down[l].astype(BF16))
    mw = (row(norm_mix[l]), w_in[l].astype(BF16), row(b_in[l]), w_dw[l].astype(F32), row(b_dw[l]),
          row(ln_conv_g[l]), row(ln_conv_b[l]), _pool_block_diag(w_pool[l]).astype(BF16), row(pool_scale[l]),
          w_out[l].astype(BF16), row(b_out[l]))
    gf = row(norm_final)

    x1p = _ffn(xp, *ffn1, tm=TM_FFN, name="ffn1_prompt")
    x1s = _ffn(xs, *ffn1, tm=xs.shape[0], name="ffn1_small")
    x1_sample, x1_meta = x1s[:nb * nt], x1s[nb * nt:]

    x2p, nc_p, np_p = _mix_prompt(x1p.reshape(bsz, seq, D_MODEL), x1_meta, mw, tm=TM_MIX)
    x2s, nc_s, np_s = _mix_sample(x1_sample, sc_tm, sp_tm, mw, nb=nb, nt=nt)

    yp = _ffn(x2p.reshape(bsz * seq, D_MODEL), *ffn2, gf, tm=TM_FFN, name="ffn2_prompt")
    ys = _ffn(x2s, *ffn2, gf, tm=nb * nt, name="ffn2_sample")

    y_prompt = yp.reshape(bsz, seq, D_MODEL)
    y_sample = ys.reshape(nt, nb, D_MODEL).transpose(1, 0, 2)
    return (y_prompt, y_sample, nc_p[None], np_p[None],
            nc_s.transpose(1, 0, 2)[None], np_s.transpose(1, 0, 2)[None])
```
